```python
import jax, jax.numpy as jnp
from jax import lax
import numpy as np

D_MODEL = 2048
BATCH = 4
SEQ = 2048
DEPTH = 1
DEC_BATCH = 128
DEC_SEQ = 8
PAST_LEN = 16384
PAGE_SIZE = 128

MIX_WIDTH = D_MODEL
P_POOL = MIX_WIDTH // 2
C_RWKV = MIX_WIDTH - P_POOL
POOL_WINDOWS = (2, 4, 8, 16)
POOL_GROUP = P_POOL // len(POOL_WINDOWS)
POOL_BUF = max(POOL_WINDOWS) - 1
HEAD_SIZE = 64
N_HEADS = C_RWKV // HEAD_SIZE
LORA_W = max(32, int(round(1.8 * C_RWKV ** 0.5 / 32)) * 32)
LORA_A = max(32, int(round(1.8 * C_RWKV ** 0.5 / 32)) * 32)
LORA_G = max(32, int(round(0.6 * C_RWKV ** 0.8 / 32)) * 32)
RW_COLS = 3 * C_RWKV + LORA_W + LORA_A + LORA_G
IN_COLS = P_POOL + RW_COLS
LNX_EPS = HEAD_SIZE * 1e-5
NORM_EPS = 1e-5
N_EXPERTS = 32
TOP_K = 4
D_FF = D_MODEL
SWIGLU_LIMIT = 7.0
SWIGLU_ALPHA = 1.702
MOE_BLOCK = 128

kernel_name = 'hybrid_pool_rwkv7_moe_adaln_step'


def rmsnorm(x, g):
    xf = x.astype(jnp.float32)
    y = xf * lax.rsqrt(jnp.mean(xf * xf, axis=-1, keepdims=True) + NORM_EPS)
    return (y * g.astype(jnp.float32)).astype(x.dtype)


def pool_mix(xp, prefix, start_pos, pool_w, pool_scale):
    B, S, _ = xp.shape
    cat = jnp.concatenate([prefix.astype(xp.dtype), xp], axis=1)
    catf = cat.astype(jnp.float32)
    cs = jnp.concatenate([jnp.zeros((B, 1, P_POOL), jnp.float32), jnp.cumsum(catf, axis=1)], axis=1)
    pos = start_pos + jnp.arange(S)
    outs = []
    for gi, w in enumerate(POOL_WINDOWS):
        lo_c, hi_c = gi * POOL_GROUP, (gi + 1) * POOL_GROUP
        hi = cs[:, POOL_BUF + 1:POOL_BUF + 1 + S, lo_c:hi_c]
        lo = cs[:, POOL_BUF + 1 - w:POOL_BUF + 1 - w + S, lo_c:hi_c]
        cnt = jnp.minimum(w, pos + 1).astype(jnp.float32)[None, :, None]
        d = (hi - lo) / cnt - catf[:, POOL_BUF:, lo_c:hi_c]
        outs.append(jnp.einsum('bsc,cd->bsd', d.astype(xp.dtype), pool_w[gi]))
    y = jnp.concatenate(outs, axis=-1) * pool_scale
    return y, cat[:, -POOL_BUF:]


def wkv_scan(state0, r, decay, k, v, kk, a):
    xs = tuple(jnp.swapaxes(t, 0, 1) for t in (r, decay, k, v, kk, a))

    def step(S, inp):
        r_t, w_t, k_t, v_t, kk_t, a_t = inp
        sa = jnp.einsum('bhij,bhj->bhi', S, -kk_t)
        S = (S * w_t[:, :, None, :] + sa[..., None] * (kk_t * a_t)[:, :, None, :]
             + v_t[..., None] * k_t[:, :, None, :])
        y = jnp.einsum('bhij,bhj->bhi', S, r_t)
        return S, y

    S_fin, ys = lax.scan(step, state0, xs)
    return S_fin, jnp.swapaxes(ys, 0, 1)


def rwkv_mix(pr, shift_prev, wkv0, prm):
    B, S, _ = pr.shape
    f32 = jnp.float32
    prev = jnp.concatenate([shift_prev[:, None].astype(pr.dtype), pr[:, :-1]], axis=1)
    xm = pr + (prev - pr) * prm['mu_shift']
    c = C_RWKV
    r, k, v, lw, la, lg = jnp.split(xm, [c, 2 * c, 3 * c, 3 * c + LORA_W, 3 * c + LORA_W + LORA_A], axis=-1)
    w_log = -jax.nn.softplus(-(prm['w0'] + jnp.tanh(lw) @ prm['w2']).astype(f32)) - 0.5
    decay = jnp.exp(-jnp.exp(w_log))
    a = jax.nn.sigmoid((prm['a0'] + la @ prm['a2']).astype(f32))
    g = (jax.nn.sigmoid(lg) @ prm['g2']).astype(f32)
    heads = lambda t: t.astype(f32).reshape(B, S, N_HEADS, HEAD_SIZE)
    kk = heads(k * prm['k_k'])
    kk = kk * lax.rsqrt(jnp.maximum(jnp.sum(kk * kk, axis=-1, keepdims=True), 1e-24))
    kh = heads(k.astype(f32) * (1.0 + (a - 1.0) * prm['k_a'].astype(f32)))
    rh, vh, ah, dh = heads(r), heads(v), heads(a), heads(decay)
    S_fin, y = wkv_scan(wkv0.astype(f32), rh, dh, kh, vh, kk, ah)
    mu = jnp.mean(y, axis=-1, keepdims=True)
    var = jnp.mean(jnp.square(y - mu), axis=-1, keepdims=True)
    yn = ((y - mu) * lax.rsqrt(var + LNX_EPS)).reshape(B, S, c)
    yn = yn * prm['lnx_g'].astype(f32) + prm['lnx_b'].astype(f32)
    bonus = (jnp.sum(rh * kh * prm['r_k'].astype(f32), axis=-1, keepdims=True) * vh).reshape(B, S, c)
    out = ((yn + bonus) * g).astype(pr.dtype)
    return out, pr[:, -1], S_fin.astype(pr.dtype)


def moe(h, router_w, router_b, w_gate_up, b_gate_up, w_down, b_down):
    T, D = h.shape
    logits = (h @ router_w + router_b).astype(jnp.float32)
    top_v, top_i = lax.top_k(logits, TOP_K)
    gates = jax.nn.softmax(top_v, axis=-1).astype(h.dtype)
    A = T * TOP_K
    e_flat = top_i.reshape(-1)
    tok_flat = jnp.repeat(jnp.arange(T, dtype=jnp.int32), TOP_K)
    g_flat = gates.reshape(-1)
    order = jnp.argsort(e_flat, stable=True)
    e_s, tok_s, g_s = e_flat[order], tok_flat[order], g_flat[order]
    counts = jnp.bincount(e_flat, length=N_EXPERTS)
    starts = jnp.cumsum(counts) - counts
    padded = ((counts + MOE_BLOCK - 1) // MOE_BLOCK) * MOE_BLOCK
    pstarts = jnp.cumsum(padded) - padded
    pends = pstarts + padded
    n_blocks = -(-A // MOE_BLOCK) + N_EXPERTS
    dest = pstarts[e_s] + (jnp.arange(A) - starts[e_s])
    slot_tok = jnp.full((n_blocks * MOE_BLOCK,), T, jnp.int32).at[dest].set(tok_s)
    slot_g = jnp.zeros((n_blocks * MOE_BLOCK,), h.dtype).at[dest].set(g_s)
    block_e = jnp.clip(jnp.searchsorted(pends, jnp.arange(n_blocks) * MOE_BLOCK, side='right'), 0, N_EXPERTS - 1)
    h_pad = jnp.concatenate([h, jnp.zeros((1, D), h.dtype)], axis=0)

    def run_block(args):
        tok_b, g_b, e_b = args
        xb = h_pad[tok_b]
        gu = xb @ w_gate_up[e_b] + b_gate_up[e_b]
        gate = jnp.minimum(gu[:, :D_FF], SWIGLU_LIMIT)
        up = jnp.clip(gu[:, D_FF:], -SWIGLU_LIMIT, SWIGLU_LIMIT)
        act = (up + 1.0) * (gate * jax.nn.sigmoid(gate * SWIGLU_ALPHA))
        return (act @ w_down[e_b] + b_down[e_b]) * g_b[:, None]

    outs = lax.map(run_block, (slot_tok.reshape(n_blocks, MOE_BLOCK), slot_g.reshape(n_blocks, MOE_BLOCK), block_e))
    y = jax.ops.segment_sum(outs.reshape(-1, D), slot_tok, num_segments=T + 1)
    return y[:T]


def layer(x, c, pool_prefix, shift_prev, wkv0, start_pos, prm):
    B, S, D = x.shape
    mod = jax.nn.silu(c) @ prm['w_ada'] + prm['b_ada']
    sh_m, sc_m, gt_m, sh_f, sc_f, gt_f = [m[:, None, :] for m in jnp.split(mod, 6, axis=-1)]
    h = rmsnorm(x, prm['norm_mix_g']) * (1.0 + sc_m) + sh_m
    p = h @ prm['w_in']
    pool_out, new_pool = pool_mix(p[..., :P_POOL], pool_prefix, start_pos, prm['pool_w'], prm['pool_scale'])
    rw_out, new_shift, new_wkv = rwkv_mix(p[..., P_POOL:], shift_prev, wkv0, prm)
    mix = jnp.concatenate([pool_out, rw_out], axis=-1) @ prm['w_out']
    x = x + gt_m * mix
    h2 = rmsnorm(x, prm['norm_ffn_g']) * (1.0 + sc_f) + sh_f
    ff = moe(h2.reshape(B * S, D), prm['router_w'], prm['router_b'], prm['w_gate_up'],
             prm['b_gate_up'], prm['w_down'], prm['b_down']).reshape(B, S, D)
    x = x + gt_f * ff
    return x, new_pool, new_shift, new_wkv


def setup_inputs(seed: int = 0) -> dict:
    key = jax.random.key(seed)
    ks = iter(list(jax.random.split(key, 48)))
    f32 = jnp.float32
    nrm = lambda shape, s: jax.random.normal(next(ks), shape, f32) * s
    L, D, C, E, F = DEPTH, D_MODEL, C_RWKV, N_EXPERTS, D_FF
    return {
        'x_prompt': nrm((BATCH, SEQ, D), 1.0),
        'x_sample': nrm((DEC_BATCH, DEC_SEQ, D), 1.0),
        'c_prompt': nrm((BATCH, D), 1.0),
        'c_sample': nrm((DEC_BATCH, D), 1.0),
        'state_pool': nrm((L, DEC_BATCH, POOL_BUF, P_POOL), 1.0),
        'state_shift': nrm((L, DEC_BATCH, RW_COLS), 1.0),
        'state_wkv': nrm((L, DEC_BATCH, N_HEADS, HEAD_SIZE, HEAD_SIZE), 0.3),
        'norm_mix_g': 1.0 + nrm((L, D), 0.1),
        'norm_ffn_g': 1.0 + nrm((L, D), 0.1),
        'final_g': 1.0 + nrm((D,), 0.1),
        'w_ada': nrm((L, D, 6 * D), 0.5 * D ** -0.5),
        'b_ada': nrm((L, 6 * D), 0.02),
        'w_in': nrm((L, D, IN_COLS), D ** -0.5),
        'pool_w': nrm((L, len(POOL_WINDOWS), POOL_GROUP, POOL_GROUP), POOL_GROUP ** -0.5),
        'pool_scale': 1.0 + nrm((L, P_POOL), 0.1),
        'mu_shift': jax.random.uniform(next(ks), (L, RW_COLS), f32),
        'w0': jax.random.uniform(next(ks), (L, C), f32, -4.0, 1.0),
        'w2': nrm((L, LORA_W, C), 0.5 * LORA_W ** -0.5),
        'a0': nrm((L, C), 0.5),
        'a2': nrm((L, LORA_A, C), 0.5 * LORA_A ** -0.5),
        'g2': nrm((L, LORA_G, C), LORA_G ** -0.5),
        'k_k': 1.0 + nrm((L, C), 0.1),
        'k_a': 1.0 + nrm((L, C), 0.1),
        'r_k': nrm((L, N_HEADS, HEAD_SIZE), 0.1),
        'lnx_g': 1.0 + nrm((L, C), 0.1),
        'lnx_b': nrm((L, C), 0.01),
        'w_out': nrm((L, MIX_WIDTH, D), MIX_WIDTH ** -0.5),
        'router_w': nrm((L, D, E), D ** -0.5),
        'router_b': nrm((L, E), 0.01),
        'w_gate_up': nrm((L, E, D, 2 * F), D ** -0.5),
        'b_gate_up': nrm((L, E, 2 * F), 0.01),
        'w_down': nrm((L, E, F, D), F ** -0.5),
        'b_down': nrm((L, E, D), 0.01),
    }


def reference(x_prompt, x_sample, c_prompt, c_sample, state_pool, state_shift, state_wkv,
              norm_mix_g, norm_ffn_g, final_g, w_ada, b_ada, w_in, pool_w, pool_scale,
              mu_shift, w0, w2, a0, a2, g2, k_k, k_a, r_k, lnx_g, lnx_b, w_out,
              router_w, router_b, w_gate_up, b_gate_up, w_down, b_down):
    Bp = x_prompt.shape[0]
    dt = x_prompt.dtype
    zero_pool = jnp.zeros((Bp, POOL_BUF, P_POOL), dt)
    zero_shift = jnp.zeros((Bp, RW_COLS), dt)
    zero_wkv = jnp.zeros((Bp, N_HEADS, HEAD_SIZE, HEAD_SIZE), dt)
    xp, xs = x_prompt, x_sample
    pp, psh, pw, sp, ssh, sw = [], [], [], [], [], []
    for l in range(DEPTH):
        prm = {
            'norm_mix_g': norm_mix_g[l], 'norm_ffn_g': norm_ffn_g[l],
            'w_ada': w_ada[l], 'b_ada': b_ada[l], 'w_in': w_in[l],
            'pool_w': pool_w[l], 'pool_scale': pool_scale[l], 'mu_shift': mu_shift[l],
            'w0': w0[l], 'w2': w2[l], 'a0': a0[l], 'a2': a2[l], 'g2': g2[l],
            'k_k': k_k[l], 'k_a': k_a[l], 'r_k': r_k[l], 'lnx_g': lnx_g[l], 'lnx_b': lnx_b[l],
            'w_out': w_out[l], 'router_w': router_w[l], 'router_b': router_b[l],
            'w_gate_up': w_gate_up[l], 'b_gate_up': b_gate_up[l],
            'w_down': w_down[l], 'b_down': b_down[l],
        }
        xp, np_pool, np_shift, np_wkv = layer(xp, c_prompt, zero_pool, zero_shift, zero_wkv, 0, prm)
        xs, ns_pool, ns_shift, ns_wkv = layer(xs, c_sample, state_pool[l], state_shift[l], state_wkv[l], PAST_LEN, prm)
        pp.append(np_pool); psh.append(np_shift); pw.append(np_wkv)
        sp.append(ns_pool); ssh.append(ns_shift); sw.append(ns_wkv)
    y_prompt = rmsnorm(xp, final_g)
    y_sample = rmsnorm(xs, final_g)
    return (y_prompt, y_sample, jnp.stack(pp), jnp.stack(psh), jnp.stack(pw),
            jnp.stack(sp), jnp.stack(ssh), jnp.stack(sw))
```

```python
import functools

import jax
import jax.numpy as jnp
from jax import lax
from jax.experimental import pallas as pl
from jax.experimental.pallas import tpu as pltpu

F32 = jnp.float32
BF16 = jnp.bfloat16
HIGHEST = lax.Precision.HIGHEST

HEAD_SIZE = 64
LANES = 128
POOL_WINDOWS = (2, 4, 8, 16)
POOL_BUF = max(POOL_WINDOWS) - 1
POOL_HALO = 16
TOP_K = 4
NORM_EPS = 1e-5
LNX_EPS = HEAD_SIZE * 1e-5
PAST_LEN = 16384
SWIGLU_LIMIT = 7.0
SWIGLU_ALPHA = 1.702
DECAY_SCALE = 0.6065306597126334
VMEM_LIMIT = 56 * 1024 * 1024
MOE_ROWS = 256


def _cparams(sem):
    return pltpu.CompilerParams(dimension_semantics=sem, vmem_limit_bytes=VMEM_LIMIT)


def _row_tile(B, S, rows):
    if S >= rows:
        assert S % rows == 0
        return 1, rows
    bb = min(B, max(1, rows // S))
    assert B % bb == 0
    return bb, S


def _split_bf16(x):
    hi = x.astype(BF16)
    lo = (x - hi.astype(F32)).astype(BF16)
    return hi, lo


def _ones_blockdiag(n):
    r = lax.broadcasted_iota(jnp.int32, (n, n), 0) // HEAD_SIZE
    c = lax.broadcasted_iota(jnp.int32, (n, n), 1) // HEAD_SIZE
    return (r == c).astype(BF16)


def _head_sum(x, ones):
    n = ones.shape[0]
    hi, lo = _split_bf16(x)
    outs = []
    for c0 in range(0, x.shape[1], n):
        outs.append(jnp.dot(hi[:, c0:c0 + n], ones, preferred_element_type=F32)
                    + jnp.dot(lo[:, c0:c0 + n], ones, preferred_element_type=F32))
    return outs[0] if len(outs) == 1 else jnp.concatenate(outs, axis=1)


def _ada_kernel(c_ref, w_ref, b_ref, o_ref):
    c = c_ref[...]
    s = c * jax.nn.sigmoid(c)
    o_ref[...] = jnp.dot(s, w_ref[...], precision=HIGHEST, preferred_element_type=F32) + b_ref[...]


def _ada(c, w, b):
    M, D = c.shape
    N = w.shape[1]
    tn = 1024 if N % 1024 == 0 else N
    return pl.pallas_call(
        _ada_kernel,
        grid=(N // tn,),
        in_specs=[pl.BlockSpec((M, D), lambda j: (0, 0)),
                  pl.BlockSpec((D, tn), lambda j: (0, j)),
                  pl.BlockSpec((1, tn), lambda j: (0, j))],
        out_specs=pl.BlockSpec((M, tn), lambda j: (0, j)),
        out_shape=jax.ShapeDtypeStruct((M, N), F32),
        compiler_params=_cparams(("arbitrary",)),
        name="ada",
    )(c, w, b.reshape(1, N))


def _in_kernel(x_ref, sh_ref, sc_ref, g_ref, w_ref, o_ref, h_scr):
    bb, ts, D = x_ref.shape

    @pl.when(pl.program_id(2) == 0)
    def _():
        x = x_ref[...]
        y = x * lax.rsqrt(jnp.mean(x * x, axis=-1, keepdims=True) + NORM_EPS) * g_ref[...]
        h = y * (1.0 + sc_ref[...]) + sh_ref[...]
        h_scr[...] = h.reshape(bb * ts, D).astype(BF16)

    o_ref[...] = jnp.dot(h_scr[...], w_ref[...], preferred_element_type=F32).reshape(o_ref.shape)


def _in_proj(x, mod, g, w_bf16, tn):
    B, S, D = x.shape
    N = w_bf16.shape[1]
    bb, ts = _row_tile(B, S, 512)
    return pl.pallas_call(
        _in_kernel,
        grid=(B // bb, S // ts, N // tn),
        in_specs=[pl.BlockSpec((bb, ts, D), lambda i, t, j: (i, t, 0)),
                  pl.BlockSpec((bb, 1, D), lambda i, t, j: (i, 0, 0)),
                  pl.BlockSpec((bb, 1, D), lambda i, t, j: (i, 0, 1)),
                  pl.BlockSpec((1, D), lambda i, t, j: (0, 0)),
                  pl.BlockSpec((D, tn), lambda i, t, j: (0, j))],
        out_specs=pl.BlockSpec((bb, ts, tn), lambda i, t, j: (i, t, j)),
        out_shape=jax.ShapeDtypeStruct((B, S, N), F32),
        scratch_shapes=[pltpu.VMEM((bb * ts, D), BF16)],
        compiler_params=_cparams(("arbitrary", "arbitrary", "arbitrary")),
        name="in_proj",
    )(x, mod, mod, g, w_bf16)


def _pool_kernel(x_ref, pre_ref, w_ref, scale_ref, o_ref, ext, dscr, *, nt, start_pos):
    bb, ts, P = x_ref.shape
    G = P // len(POOL_WINDOWS)
    H = POOL_HALO
    t = pl.program_id(1)

    if nt > 1:
        @pl.when(t > 0)
        def _():
            ext[:, 0:H, :] = ext[:, ts:ts + H, :]

    @pl.when(t == 0)
    def _():
        ext[:, 0:H - POOL_BUF, :] = jnp.zeros((bb, H - POOL_BUF, P), F32)
        ext[:, H - POOL_BUF:H, :] = pre_ref[...]

    ext[:, H:, :] = x_ref[...]
    seen = lax.broadcasted_iota(jnp.int32, (ts, G), 0) + (t * ts + start_pos + 1)
    for b in range(bb):
        for gi, w in enumerate(POOL_WINDOWS):
            lo, hi = gi * G, (gi + 1) * G
            cur = ext[b, H:H + ts, lo:hi]
            acc = cur
            for k in range(1, w):
                acc = acc + ext[b, H - k:H - k + ts, lo:hi]
            cnt = jnp.minimum(w, seen).astype(F32)
            dscr[b * ts:(b + 1) * ts, lo:hi] = acc / cnt - cur
    for gi in range(len(POOL_WINDOWS)):
        lo, hi = gi * G, (gi + 1) * G
        y = jnp.dot(dscr[:, lo:hi].astype(BF16), w_ref[gi].astype(BF16), preferred_element_type=F32)
        o_ref[:, :, lo:hi] = (y * scale_ref[:, lo:hi]).reshape(bb, ts, G)


def _pool(p, prefix, pool_w, pool_scale, start_pos):
    B, S, _ = p.shape
    P = prefix.shape[-1]
    bb, ts = _row_tile(B, S, 512)
    bb = min(bb, 8)
    nt = S // ts
    assert nt == 1 or ts >= POOL_HALO
    ng, G, _ = pool_w.shape
    return pl.pallas_call(
        functools.partial(_pool_kernel, nt=nt, start_pos=start_pos),
        grid=(B // bb, nt),
        in_specs=[pl.BlockSpec((bb, ts, P), lambda i, t: (i, t, 0)),
                  pl.BlockSpec((bb, POOL_BUF, P), lambda i, t: (i, 0, 0)),
                  pl.BlockSpec((ng, G, G), lambda i, t: (0, 0, 0)),
                  pl.BlockSpec((1, P), lambda i, t: (0, 0))],
        out_specs=pl.BlockSpec((bb, ts, P), lambda i, t: (i, t, 0)),
        out_shape=jax.ShapeDtypeStruct((B, S, P), F32),
        scratch_shapes=[pltpu.VMEM((bb, POOL_HALO + ts, P), F32), pltpu.VMEM((bb * ts, P), F32)],
        compiler_params=_cparams(("arbitrary", "arbitrary")),
        name="pool",
    )(p, prefix, pool_w, pool_scale.reshape(1, P))


def _prep_kernel(r_ref, k_ref, v_ref, l_ref, sr_ref, sk_ref, sv_ref, sl_ref, mr_ref, mk_ref, mv_ref, ml_ref,
                 w0_ref, w2_ref, a0_ref, a2_ref, g2_ref, kkw_ref, kaw_ref, rkw_ref,
                 kk_o, q_o, w_o, kka_o, kh_o, v_o, c2_o, bc_o, g_o, carry, *, nt, gs):
    bb, ts, C = r_ref.shape
    LW = l_ref.shape[-1]
    M = bb * ts
    t = pl.program_id(1)
    first_row = lax.broadcasted_iota(jnp.int32, (bb, ts, 1), 1) == 0

    def shifted_mix(x_ref, s_ref, m_ref, c0, width):
        x = x_ref[...]
        first = s_ref[...]
        if nt > 1:
            first = jnp.where(t == 0, first, carry[:, :, c0:c0 + width])
            carry[:, :, c0:c0 + width] = x[:, ts - 1:ts, :]
        prev = jnp.where(first_row, first, pltpu.roll(x, 1, axis=1))
        return (x + (prev - x) * m_ref[...]).reshape(M, width)

    r = shifted_mix(r_ref, sr_ref, mr_ref, 0, C)
    k = shifted_mix(k_ref, sk_ref, mk_ref, C, C)
    v = shifted_mix(v_ref, sv_ref, mv_ref, 2 * C, C)
    lo = shifted_mix(l_ref, sl_ref, ml_ref, 3 * C, LW)

    K1 = w2_ref.shape[0]
    wpre = w0_ref[...] + jnp.dot(jnp.tanh(lo[:, :K1]), w2_ref[...], precision=HIGHEST, preferred_element_type=F32)
    decay = jnp.exp(-DECAY_SCALE * jax.nn.sigmoid(wpre))
    a = jax.nn.sigmoid(a0_ref[...] + jnp.dot(lo[:, :K1], a2_ref[...], precision=HIGHEST, preferred_element_type=F32))
    g = jnp.dot(jax.nn.sigmoid(lo[:, gs:]), g2_ref[...], precision=HIGHEST, preferred_element_type=F32)

    ones = _ones_blockdiag(256 if C % 256 == 0 else LANES)
    kk = k * kkw_ref[...]
    kk = kk * lax.rsqrt(jnp.maximum(_head_sum(kk * kk, ones), 1e-24))
    kh = k * (1.0 + (a - 1.0) * kaw_ref[...])
    kka = kk * a
    c1 = _head_sum(kka * r, ones)
    rk = kh * r
    shp = (bb, ts, C)
    kk_o[...] = kk.reshape(shp)
    q_o[...] = (decay * r - kk * c1).reshape(shp)
    w_o[...] = decay.reshape(shp)
    kka_o[...] = kka.reshape(shp)
    kh_o[...] = kh.reshape(shp)
    v_o[...] = v.reshape(shp)
    c2_o[...] = _head_sum(rk, ones).reshape(shp)
    bc_o[...] = _head_sum(rk * rkw_ref[...], ones).reshape(shp)
    g_o[...] = g.reshape(shp)


def _prep(p, shift_pad, mu_pad, C, LW, w0, w2p, a0, a2p, g2p, k_k, k_a, r_k, gs):
    B, S, _ = p.shape
    bb, ts = _row_tile(B, S, 256)
    nt = S // ts
    lb = (4 * C) // LW

    def col(width, idx):
        return (pl.BlockSpec((bb, ts, width), lambda i, t: (i, t, idx)),
                pl.BlockSpec((bb, 1, width), lambda i, t: (i, 0, idx)),
                pl.BlockSpec((1, width), lambda i, t: (0, idx)))

    cols = [col(C, 1), col(C, 2), col(C, 3), col(LW, lb)]
    vec = pl.BlockSpec((1, C), lambda i, t: (0, 0))
    mat = lambda m: pl.BlockSpec(m.shape, lambda i, t: (0, 0))
    out = pl.BlockSpec((bb, ts, C), lambda i, t: (i, t, 0))
    return pl.pallas_call(
        functools.partial(_prep_kernel, nt=nt, gs=gs),
        grid=(B // bb, nt),
        in_specs=[c[0] for c in cols] + [c[1] for c in cols] + [c[2] for c in cols]
                 + [vec, mat(w2p), vec, mat(a2p), mat(g2p), vec, vec, vec],
        out_specs=[out] * 9,
        out_shape=[jax.ShapeDtypeStruct((B, S, C), F32)] * 9,
        scratch_shapes=[pltpu.VMEM((bb, 1, 3 * C + LW), F32)],
        compiler_params=_cparams(("arbitrary", "arbitrary")),
        name="rwkv_prep",
    )(p, p, p, p, shift_pad, shift_pad, shift_pad, shift_pad, mu_pad, mu_pad, mu_pad, mu_pad,
      w0.reshape(1, C), w2p, a0.reshape(1, C), a2p, g2p, k_k.reshape(1, C), k_a.reshape(1, C), r_k.reshape(1, C))


def _scan_kernel(kk_ref, q_ref, w_ref, kka_ref, kh_ref, v_ref, s0_ref, y_ref, s_ref):
    bb, ts, C = kk_ref.shape
    HP = C // LANES

    @pl.when(pl.program_id(1) == 0)
    def _():
        s_ref[...] = s0_ref[...]

    y_ref[...] = jnp.zeros(y_ref.shape, F32)
    ones = _ones_blockdiag(LANES)
    row = lax.broadcasted_iota(jnp.int32, (HEAD_SIZE, LANES), 0)
    lane = lax.broadcasted_iota(jnp.int32, (HEAD_SIZE, LANES), 1)
    eye = (row == lane % HEAD_SIZE).astype(F32)
    row8 = lax.broadcasted_iota(jnp.int32, (8, LANES), 0)

    def step(t, carry):
        kk_t = kk_ref[:, pl.ds(t, 1), :]
        q_t = q_ref[:, pl.ds(t, 1), :]
        w_t = w_ref[:, pl.ds(t, 1), :]
        kka_t = kka_ref[:, pl.ds(t, 1), :]
        kh_t = kh_ref[:, pl.ds(t, 1), :]
        v_t = v_ref[:, pl.ds(t, 1), :]
        t8 = pl.multiple_of((t // 8) * 8, 8)
        this_row = row8 == t - t8
        for b in range(bb):
            for hp in range(HP):
                sl = slice(hp * LANES, (hp + 1) * LANES)
                S = s_ref[b, hp]
                lhs = jnp.concatenate([S * kk_t[b, :, sl], S * q_t[b, :, sl], eye * v_t[b, :, sl]], axis=0)
                hi, lo = _split_bf16(lhs)
                red = (jnp.dot(hi, ones, preferred_element_type=F32)
                       + jnp.dot(lo, ones, preferred_element_type=F32))
                sa = red[0:HEAD_SIZE]
                yb = red[HEAD_SIZE:2 * HEAD_SIZE]
                vcol = red[2 * HEAD_SIZE:]
                s_ref[b, hp] = S * w_t[b, :, sl] - sa * kka_t[b, :, sl] + vcol * kh_t[b, :, sl]
                y_row = jnp.sum(yb * eye, axis=0, keepdims=True)
                y_ref[b, pl.ds(t8, 8), sl] = jnp.where(this_row, y_row, y_ref[b, pl.ds(t8, 8), sl])
        return carry

    lax.fori_loop(0, ts, step, 0)


def _scan(kk, q, w, kka, kh, v, s0, bb, ts):
    B, S, C = kk.shape
    HP = C // LANES
    row = pl.BlockSpec((bb, ts, C), lambda i, t: (i, t, 0))
    st = pl.BlockSpec((bb, HP, HEAD_SIZE, LANES), lambda i, t: (i, 0, 0, 0))
    return pl.pallas_call(
        _scan_kernel,
        grid=(B // bb, S // ts),
        in_specs=[row] * 6 + [st],
        out_specs=[row, st],
        out_shape=[jax.ShapeDtypeStruct((B, S, C), F32), jax.ShapeDtypeStruct(s0.shape, F32)],
        compiler_params=_cparams(("arbitrary", "arbitrary")),
        name="wkv_scan",
    )(kk, q, w, kka, kh, v, s0)


def _pair_heads(s):
    B, H, N, _ = s.shape
    return s.reshape(B, H // 2, 2, N, N).transpose(0, 1, 3, 2, 4).reshape(B, H // 2, N, 2 * N)


def _unpair_heads(s):
    B, HP, N, _ = s.shape
    return s.reshape(B, HP, N, 2, N).transpose(0, 1, 3, 2, 4).reshape(B, 2 * HP, N, N)


def _out_kernel(x_ref, gtm_ref, shf_ref, scf_ref, pool_ref, y0_ref, v_ref, c2_ref, bc_ref, g_ref,
                lng_ref, lnb_ref, wo_ref, nf_ref, rw_ref, rb_ref,
                x1_ref, h2_ref, ti_ref, tg_ref):
    bb, ts, D = x_ref.shape
    C = y0_ref.shape[-1]
    P = pool_ref.shape[-1]
    M = bb * ts
    E = rw_ref.shape[-1]

    ones = _ones_blockdiag(256 if C % 256 == 0 else LANES)
    v = v_ref[...].reshape(M, C)
    y = y0_ref[...].reshape(M, C) + v * c2_ref[...].reshape(M, C)
    mu = _head_sum(y, ones) * (1.0 / HEAD_SIZE)
    yc = y - mu
    var = _head_sum(yc * yc, ones) * (1.0 / HEAD_SIZE)
    yn = yc * lax.rsqrt(var + LNX_EPS) * lng_ref[...] + lnb_ref[...]
    rw = (yn + bc_ref[...].reshape(M, C) * v) * g_ref[...].reshape(M, C)

    mix = (jnp.dot(pool_ref[...].reshape(M, P).astype(BF16), wo_ref[0:P, :], preferred_element_type=F32)
           + jnp.dot(rw.astype(BF16), wo_ref[P:P + C, :], preferred_element_type=F32))
    x1 = x_ref[...] + gtm_ref[...] * mix.reshape(bb, ts, D)
    x1_ref[...] = x1
    hn = x1 * lax.rsqrt(jnp.mean(x1 * x1, axis=-1, keepdims=True) + NORM_EPS) * nf_ref[...]
    h2 = (hn * (1.0 + scf_ref[...]) + shf_ref[...]).reshape(M, D)
    h2_ref[...] = h2.astype(BF16).reshape(bb, ts, D)

    logits = jnp.dot(h2, rw_ref[...], precision=HIGHEST, preferred_element_type=F32) + rb_ref[...]
    eidx = lax.broadcasted_iota(jnp.int32, (M, E), 1)
    slot = lax.broadcasted_iota(jnp.int32, (M, LANES), 1)
    top_i = jnp.zeros((M, LANES), jnp.int32)
    top_v = jnp.full((M, LANES), -jnp.inf, F32)
    for kth in range(TOP_K):
        m = jnp.max(logits, axis=-1, keepdims=True)
        idx = jnp.min(jnp.where(logits == m, eidx, E), axis=-1, keepdims=True)
        top_i = jnp.where(slot == kth, idx, top_i)
        top_v = jnp.where(slot == kth, m, top_v)
        logits = jnp.where(eidx == idx, -jnp.inf, logits)
    ex = jnp.exp(top_v - jnp.max(top_v, axis=-1, keepdims=True))
    ti_ref[...] = top_i.reshape(bb, ts, LANES)
    tg_ref[...] = (ex / jnp.sum(ex, axis=-1, keepdims=True)).reshape(bb, ts, LANES)


def _out_proj(x, mod, pool_out, y0, v, c2, bc, g, lnx_g, lnx_b, wo_bf16, norm_g, router_w, router_b):
    B, S, D = x.shape
    C = y0.shape[-1]
    P = pool_out.shape[-1]
    E = router_w.shape[-1]
    bb, ts = _row_tile(B, S, 256)
    tok = lambda width: pl.BlockSpec((bb, ts, width), lambda i, t: (i, t, 0))
    modc = lambda idx: pl.BlockSpec((bb, 1, D), lambda i, t: (i, 0, idx))
    full = lambda a: pl.BlockSpec(a.shape, lambda i, t: (0,) * a.ndim)
    lnx_g, lnx_b, norm_g, router_b = (lnx_g.reshape(1, C), lnx_b.reshape(1, C), norm_g.reshape(1, D),
                                      router_b.reshape(1, E))
    return pl.pallas_call(
        _out_kernel,
        grid=(B // bb, S // ts),
        in_specs=[tok(D), modc(2), modc(3), modc(4), tok(P)] + [tok(C)] * 5
                 + [full(lnx_g), full(lnx_b), full(wo_bf16), full(norm_g), full(router_w), full(router_b)],
        out_specs=[tok(D), tok(D), tok(LANES), tok(LANES)],
        out_shape=[jax.ShapeDtypeStruct((B, S, D), F32), jax.ShapeDtypeStruct((B, S, D), BF16),
                   jax.ShapeDtypeStruct((B, S, LANES), jnp.int32), jax.ShapeDtypeStruct((B, S, LANES), F32)],
        compiler_params=_cparams(("arbitrary", "arbitrary")),
        name="out_proj",
    )(x, mod, mod, mod, pool_out, y0, v, c2, bc, g, lnx_g, lnx_b, wo_bf16, norm_g, router_w, router_b)


def _expert_changed(be_ref, b):
    return (b == 0) | (be_ref[b] != be_ref[jnp.maximum(b - 1, 0)])


def _moe1_kernel(be_ref, nu_ref, x_ref, wg_ref, wu_ref, bg_ref, bu_ref, h_ref, wg_s, wu_s):
    b = pl.program_id(1)

    @pl.when(b < nu_ref[0])
    def _():
        @pl.when(_expert_changed(be_ref, b))
        def _():
            wg_s[...] = wg_ref[0].astype(BF16)
            wu_s[...] = wu_ref[0].astype(BF16)

        x = x_ref[...]
        gate = jnp.dot(x, wg_s[...], preferred_element_type=F32) + bg_ref[0]
        up = jnp.dot(x, wu_s[...], preferred_element_type=F32) + bu_ref[0]
        gate = jnp.minimum(gate, SWIGLU_LIMIT)
        up = jnp.clip(up, -SWIGLU_LIMIT, SWIGLU_LIMIT)
        h_ref[...] = ((up + 1.0) * (gate * jax.nn.sigmoid(gate * SWIGLU_ALPHA))).astype(BF16)


def _moe2_kernel(be_ref, nu_ref, h_ref, wd_ref, bd_ref, y_ref, wd_s):
    b = pl.program_id(1)

    @pl.when(b < nu_ref[0])
    def _():
        @pl.when(_expert_changed(be_ref, b))
        def _():
            wd_s[...] = wd_ref[0].astype(BF16)

        y_ref[...] = jnp.dot(h_ref[...], wd_s[...], preferred_element_type=F32) + bd_ref[0]


def _moe(xs, block_e, nused, w_gate_up, b_gate_up, w_down, b_down):
    R, D = xs.shape
    E, _, F2 = w_gate_up.shape
    F = F2 // 2
    tm = MOE_ROWS
    NB = R // tm
    tf = 512 if F % 512 == 0 else F
    NF = F // tf
    blk = lambda b, nu: jnp.minimum(b, nu[0] - 1)
    bgu = b_gate_up.reshape(E, 1, F2)
    h = pl.pallas_call(
        _moe1_kernel,
        grid_spec=pltpu.PrefetchScalarGridSpec(
            num_scalar_prefetch=2,
            grid=(NF, NB),
            in_specs=[pl.BlockSpec((tm, D), lambda f, b, be, nu: (blk(b, nu), 0)),
                      pl.BlockSpec((1, D, tf), lambda f, b, be, nu: (be[blk(b, nu)], 0, f)),
                      pl.BlockSpec((1, D, tf), lambda f, b, be, nu: (be[blk(b, nu)], 0, NF + f)),
                      pl.BlockSpec((1, 1, tf), lambda f, b, be, nu: (be[blk(b, nu)], 0, f)),
                      pl.BlockSpec((1, 1, tf), lambda f, b, be, nu: (be[blk(b, nu)], 0, NF + f))],
            out_specs=pl.BlockSpec((tm, tf), lambda f, b, be, nu: (blk(b, nu), f)),
            scratch_shapes=[pltpu.VMEM((D, tf), BF16), pltpu.VMEM((D, tf), BF16)]),
        out_shape=jax.ShapeDtypeStruct((R, F), BF16),
        compiler_params=_cparams(("arbitrary", "arbitrary")),
        name="moe_up",
    )(block_e, nused, xs, w_gate_up, w_gate_up, bgu, bgu)
    tn = 512 if D % 512 == 0 else D
    NN = D // tn
    return pl.pallas_call(
        _moe2_kernel,
        grid_spec=pltpu.PrefetchScalarGridSpec(
            num_scalar_prefetch=2,
            grid=(NN, NB),
            in_specs=[pl.BlockSpec((tm, F), lambda n, b, be, nu: (blk(b, nu), 0)),
                      pl.BlockSpec((1, F, tn), lambda n, b, be, nu: (be[blk(b, nu)], 0, n)),
                      pl.BlockSpec((1, 1, tn), lambda n, b, be, nu: (be[blk(b, nu)], 0, n))],
            out_specs=pl.BlockSpec((tm, tn), lambda n, b, be, nu: (blk(b, nu), n)),
            scratch_shapes=[pltpu.VMEM((F, tn), BF16)]),
        out_shape=jax.ShapeDtypeStruct((R, D), F32),
        compiler_params=_cparams(("arbitrary", "arbitrary")),
        name="moe_down",
    )(block_e, nused, h, w_down, b_down.reshape(E, 1, D))


def _route(top_i, n_experts):
    T = top_i.shape[0]
    A = T * TOP_K
    tm = MOE_ROWS
    NB = -(-A // tm) + n_experts
    e_flat = top_i.reshape(-1)
    order = jnp.argsort(e_flat, stable=True).astype(jnp.int32)
    e_s = e_flat[order]
    counts = jnp.bincount(e_flat, length=n_experts).astype(jnp.int32)
    starts = jnp.cumsum(counts) - counts
    padded = ((counts + tm - 1) // tm) * tm
    pends = jnp.cumsum(padded)
    pstarts = pends - padded
    dest = pstarts[e_s] + (jnp.arange(A, dtype=jnp.int32) - starts[e_s])
    slot_tok = jnp.zeros((NB * tm,), jnp.int32).at[dest].set(order // TOP_K)
    pos = jnp.zeros((A,), jnp.int32).at[order].set(dest)
    block_e = jnp.clip(jnp.searchsorted(pends, jnp.arange(NB, dtype=jnp.int32) * tm, side='right'),
                       0, n_experts - 1).astype(jnp.int32)
    nused = (pends[-1] // tm).astype(jnp.int32).reshape(1)
    return slot_tok, pos.reshape(T, TOP_K), block_e, nused


def _final_kernel(x1_ref, gtf_ref, yg_ref, tg_ref, fg_ref, o_ref):
    bb, ts, D = x1_ref.shape
    yg = yg_ref[...]
    tg = tg_ref[...]
    ff = tg[:, :, 0:1] * yg[:, :, 0:D]
    for k in range(1, TOP_K):
        ff = ff + tg[:, :, k:k + 1] * yg[:, :, k * D:(k + 1) * D]
    x2 = x1_ref[...] + gtf_ref[...] * ff
    o_ref[...] = x2 * lax.rsqrt(jnp.mean(x2 * x2, axis=-1, keepdims=True) + NORM_EPS) * fg_ref[...]


def _final(x1, mod, yg, tg, final_g):
    B, S, D = x1.shape
    bb, ts = _row_tile(B, S, 256)
    tok = lambda width: pl.BlockSpec((bb, ts, width), lambda i, t: (i, t, 0))
    return pl.pallas_call(
        _final_kernel,
        grid=(B // bb, S // ts),
        in_specs=[tok(D), pl.BlockSpec((bb, 1, D), lambda i, t: (i, 0, 5)), tok(TOP_K * D), tok(LANES),
                  pl.BlockSpec((1, D), lambda i, t: (0, 0))],
        out_specs=tok(D),
        out_shape=jax.ShapeDtypeStruct((B, S, D), F32),
        compiler_params=_cparams(("arbitrary", "arbitrary")),
        name="final",
    )(x1, mod, yg, tg, final_g.reshape(1, D))


def _mixer(x, mod, prefix, shift_prev, wkv0, start_pos, prm, scan_tile):
    B, S, D = x.shape
    C, P, LW, RW = prm['C'], prm['P'], prm['LW'], prm['RW']
    p = _in_proj(x, mod, prm['norm_mix_g'], prm['w_in'], prm['tn_in'])
    pool_out = _pool(p, prefix, prm['pool_w'], prm['pool_scale'], start_pos)
    shift_pad = jnp.pad(shift_prev, ((0, 0), (P, p.shape[-1] - P - RW)))[:, None, :]
    kk, q, w, kka, kh, v, c2, bc, g = _prep(p, shift_pad, prm['mu_pad'], C, LW, prm['w0'], prm['w2p'], prm['a0'],
                                            prm['a2p'], prm['g2p'], prm['k_k'], prm['k_a'], prm['r_k'], prm['gs'])
    y0, s_fin = _scan(kk, q, w, kka, kh, v, _pair_heads(wkv0), *scan_tile)
    x1, h2, top_i, top_g = _out_proj(x, mod, pool_out, y0, v, c2, bc, g, prm['lnx_g'], prm['lnx_b'], prm['w_out'],
                                     prm['norm_ffn_g'], prm['router_w'], prm['router_b'])
    new_pool = jnp.concatenate([prefix, p[:, :, :P]], axis=1)[:, -POOL_BUF:]
    new_shift = p[:, -1, P:P + RW]
    return x1, h2, top_i, top_g, new_pool, new_shift, _unpair_heads(s_fin)


def kernel(x_prompt, x_sample, c_prompt, c_sample, state_pool, state_shift, state_wkv, norm_mix_g, norm_ffn_g, final_g, w_ada, b_ada, w_in, pool_w, pool_scale, mu_shift, w0, w2, a0, a2, g2, k_k, k_a, r_k, lnx_g, lnx_b, w_out, router_w, router_b, w_gate_up, b_gate_up, w_down, b_down):
    Bp, Sp, D = x_prompt.shape
    Bs, Ss, _ = x_sample.shape
    depth = w_in.shape[0]
    P = pool_w.shape[1] * pool_w.shape[2]
    C = w0.shape[1]
    H = C // HEAD_SIZE
    RW = mu_shift.shape[1]
    lw, la, lg = w2.shape[1], a2.shape[1], g2.shape[1]
    assert P == C and C % LANES == 0 and RW == 3 * C + lw + la + lg
    E = router_w.shape[-1]
    LW = next(n for n in range(-(-(lw + la + lg) // LANES) * LANES, 4 * C + 1, LANES) if (4 * C) % n == 0)
    NP = 4 * C + LW
    tn_in = NP // 3 if NP % (3 * LANES) == 0 else NP
    K1 = min(LW, -(-(lw + la) // LANES) * LANES)
    gs = ((lw + la) // LANES) * LANES

    xp, xs = x_prompt, x_sample
    outs = [[] for _ in range(6)]
    for l in range(depth):
        prm = {
            'C': C, 'P': P, 'LW': LW, 'RW': RW, 'tn_in': tn_in, 'gs': gs,
            'norm_mix_g': norm_mix_g[l].reshape(1, D), 'norm_ffn_g': norm_ffn_g[l],
            'w_in': jnp.pad(w_in[l].astype(BF16), ((0, 0), (0, NP - P - RW))),
            'pool_w': pool_w[l], 'pool_scale': pool_scale[l],
            'mu_pad': jnp.pad(mu_shift[l], (P, NP - P - RW)).reshape(1, NP),
            'w0': w0[l], 'a0': a0[l], 'k_k': k_k[l], 'k_a': k_a[l], 'r_k': r_k[l].reshape(C),
            'w2p': jnp.pad(w2[l], ((0, K1 - lw), (0, 0))),
            'a2p': jnp.pad(a2[l], ((lw, K1 - lw - la), (0, 0))),
            'g2p': jnp.pad(g2[l], ((lw + la - gs, LW - lw - la - lg), (0, 0))),
            'lnx_g': lnx_g[l], 'lnx_b': lnx_b[l], 'w_out': w_out[l].astype(BF16),
            'router_w': router_w[l], 'router_b': router_b[l],
        }
        mod = _ada(jnp.concatenate([c_prompt, c_sample], axis=0), w_ada[l], b_ada[l])
        mod_p, mod_s = mod[:Bp, None, :], mod[Bp:, None, :]
        zero_pool = jnp.zeros((Bp, POOL_BUF, P), F32)
        zero_shift = jnp.zeros((Bp, RW), F32)
        zero_wkv = jnp.zeros((Bp, H, HEAD_SIZE, HEAD_SIZE), F32)
        rp = _mixer(xp, mod_p, zero_pool, zero_shift, zero_wkv, 0, prm, (Bp, min(Sp, 128)))
        rs = _mixer(xs, mod_s, state_pool[l], state_shift[l], state_wkv[l], PAST_LEN, prm, (min(Bs, 8), Ss))

        Tp, Ts = Bp * Sp, Bs * Ss
        h2 = jnp.concatenate([rp[1].reshape(Tp, D), rs[1].reshape(Ts, D)], axis=0)
        top_i = jnp.concatenate([rp[2].reshape(Tp, LANES), rs[2].reshape(Ts, LANES)], axis=0)[:, :TOP_K]
        slot_tok, pos, block_e, nused = _route(top_i, E)
        y_slots = _moe(jnp.take(h2, slot_tok, axis=0), block_e, nused, w_gate_up[l], b_gate_up[l], w_down[l], b_down[l])
        yg_p = jnp.take(y_slots, pos[:Tp].reshape(-1), axis=0).reshape(Bp, Sp, TOP_K * D)
        yg_s = jnp.take(y_slots, pos[Tp:].reshape(-1), axis=0).reshape(Bs, Ss, TOP_K * D)
        last = l == depth - 1
        fg = final_g if last else jnp.ones_like(final_g)
        assert last, "deeper stacks need the un-normalised residual carried between layers"
        xp = _final(rp[0], mod_p, yg_p, rp[3], fg)
        xs = _final(rs[0], mod_s, yg_s, rs[3], fg)
        for lst, val in zip(outs, (rp[4], rp[5], rp[6], rs[4], rs[5], rs[6])):
            lst.append(val)
    return (xp, xs) + tuple(jnp.stack(o) for o in outs)
```

```python
import functools

import jax
import jax.numpy as jnp
from jax import lax
from jax.experimental import pallas as pl
from jax.experimental.pallas import tpu as pltpu

F32 = jnp.float32
BF16 = jnp.bfloat16
HIGHEST = lax.Precision.HIGHEST

HEAD_SIZE = 64
LANES = 128
POOL_WINDOWS = (2, 4, 8, 16)
POOL_BUF = max(POOL_WINDOWS) - 1
POOL_HALO = 16
TOP_K = 4
NORM_EPS = 1e-5
LNX_EPS = HEAD_SIZE * 1e-5
PAST_LEN = 16384
SWIGLU_LIMIT = 7.0
SWIGLU_ALPHA = 1.702
DECAY_SCALE = 0.6065306597126334
VMEM_LIMIT = 56 * 1024 * 1024
MOE_ROWS = 256


def _cparams(sem):
    return pltpu.CompilerParams(dimension_semantics=sem, vmem_limit_bytes=VMEM_LIMIT)


def _row_tile(B, S, rows):
    if S >= rows:
        assert S % rows == 0
        return 1, rows
    bb = min(B, max(1, rows // S))
    assert B % bb == 0
    return bb, S


def _split_bf16(x):
    hi = x.astype(BF16)
    lo = (x - hi.astype(F32)).astype(BF16)
    return hi, lo


def _ones_blockdiag(n):
    r = lax.broadcasted_iota(jnp.int32, (n, n), 0) // HEAD_SIZE
    c = lax.broadcasted_iota(jnp.int32, (n, n), 1) // HEAD_SIZE
    return (r == c).astype(BF16)


def _head_sum(x, ones):
    n = ones.shape[0]
    hi, lo = _split_bf16(x)
    outs = []
    for c0 in range(0, x.shape[1], n):
        outs.append(jnp.dot(hi[:, c0:c0 + n], ones, preferred_element_type=F32)
                    + jnp.dot(lo[:, c0:c0 + n], ones, preferred_element_type=F32))
    return outs[0] if len(outs) == 1 else jnp.concatenate(outs, axis=1)


def _ada_kernel(c_ref, w_ref, b_ref, o_ref):
    c = c_ref[...]
    s = c * jax.nn.sigmoid(c)
    o_ref[...] = jnp.dot(s, w_ref[...], precision=HIGHEST, preferred_element_type=F32) + b_ref[...]


def _ada(c, w, b):
    M, D = c.shape
    N = w.shape[1]
    tn = 1024 if N % 1024 == 0 else N
    return pl.pallas_call(
        _ada_kernel,
        grid=(N // tn,),
        in_specs=[pl.BlockSpec((M, D), lambda j: (0, 0)),
                  pl.BlockSpec((D, tn), lambda j: (0, j)),
                  pl.BlockSpec((1, tn), lambda j: (0, j))],
        out_specs=pl.BlockSpec((M, tn), lambda j: (0, j)),
        out_shape=jax.ShapeDtypeStruct((M, N), F32),
        compiler_params=_cparams(("arbitrary",)),
        name="ada",
    )(c, w, b.reshape(1, N))


def _in_kernel(x_ref, sh_ref, sc_ref, g_ref, w_ref, o_ref, h_scr):
    bb, ts, D = x_ref.shape

    @pl.when(pl.program_id(2) == 0)
    def _():
        x = x_ref[...]
        y = x * lax.rsqrt(jnp.mean(x * x, axis=-1, keepdims=True) + NORM_EPS) * g_ref[...]
        h = y * (1.0 + sc_ref[...]) + sh_ref[...]
        h_scr[...] = h.reshape(bb * ts, D).astype(BF16)

    o_ref[...] = jnp.dot(h_scr[...], w_ref[...], preferred_element_type=F32).reshape(o_ref.shape)


def _in_proj(x, mod, g, w_bf16, tn):
    B, S, D = x.shape
    N = w_bf16.shape[1]
    bb, ts = _row_tile(B, S, 512)
    return pl.pallas_call(
        _in_kernel,
        grid=(B // bb, S // ts, N // tn),
        in_specs=[pl.BlockSpec((bb, ts, D), lambda i, t, j: (i, t, 0)),
                  pl.BlockSpec((bb, 1, D), lambda i, t, j: (i, 0, 0)),
                  pl.BlockSpec((bb, 1, D), lambda i, t, j: (i, 0, 1)),
                  pl.BlockSpec((1, D), lambda i, t, j: (0, 0)),
                  pl.BlockSpec((D, tn), lambda i, t, j: (0, j))],
        out_specs=pl.BlockSpec((bb, ts, tn), lambda i, t, j: (i, t, j)),
        out_shape=jax.ShapeDtypeStruct((B, S, N), F32),
        scratch_shapes=[pltpu.VMEM((bb * ts, D), BF16)],
        compiler_params=_cparams(("arbitrary", "arbitrary", "arbitrary")),
        name="in_proj",
    )(x, mod, mod, g, w_bf16)


def _pool_kernel(x_ref, pre_ref, w_ref, scale_ref, o_ref, ext, dscr, *, nt, start_pos):
    bb, ts, P = x_ref.shape
    G = P // len(POOL_WINDOWS)
    H = POOL_HALO
    t = pl.program_id(1)

    if nt > 1:
        @pl.when(t > 0)
        def _():
            ext[:, 0:H, :] = ext[:, ts:ts + H, :]

    @pl.when(t == 0)
    def _():
        ext[:, 0:H - POOL_BUF, :] = jnp.zeros((bb, H - POOL_BUF, P), F32)
        ext[:, H - POOL_BUF:H, :] = pre_ref[...]

    ext[:, H:, :] = x_ref[...]
    seen = lax.broadcasted_iota(jnp.int32, (ts, G), 0) + (t * ts + start_pos + 1)
    for b in range(bb):
        for gi, w in enumerate(POOL_WINDOWS):
            lo, hi = gi * G, (gi + 1) * G
            cur = ext[b, H:H + ts, lo:hi]
            acc = cur
            for k in range(1, w):
                acc = acc + ext[b, H - k:H - k + ts, lo:hi]
            cnt = jnp.minimum(w, seen).astype(F32)
            dscr[b * ts:(b + 1) * ts, lo:hi] = acc / cnt - cur
    for gi in range(len(POOL_WINDOWS)):
        lo, hi = gi * G, (gi + 1) * G
        y = jnp.dot(dscr[:, lo:hi].astype(BF16), w_ref[gi].astype(BF16), preferred_element_type=F32)
        o_ref[:, :, lo:hi] = (y * scale_ref[:, lo:hi]).reshape(bb, ts, G)


def _pool(p, prefix, pool_w, pool_scale, start_pos):
    B, S, _ = p.shape
    P = prefix.shape[-1]
    bb, ts = _row_tile(B, S, 512)
    bb = min(bb, 8)
    nt = S // ts
    assert nt == 1 or ts >= POOL_HALO
    ng, G, _ = pool_w.shape
    return pl.pallas_call(
        functools.partial(_pool_kernel, nt=nt, start_pos=start_pos),
        grid=(B // bb, nt),
        in_specs=[pl.BlockSpec((bb, ts, P), lambda i, t: (i, t, 0)),
                  pl.BlockSpec((bb, POOL_BUF, P), lambda i, t: (i, 0, 0)),
                  pl.BlockSpec((ng, G, G), lambda i, t: (0, 0, 0)),
                  pl.BlockSpec((1, P), lambda i, t: (0, 0))],
        out_specs=pl.BlockSpec((bb, ts, P), lambda i, t: (i, t, 0)),
        out_shape=jax.ShapeDtypeStruct((B, S, P), F32),
        scratch_shapes=[pltpu.VMEM((bb, POOL_HALO + ts, P), F32), pltpu.VMEM((bb * ts, P), F32)],
        compiler_params=_cparams(("arbitrary", "arbitrary")),
        name="pool",
    )(p, prefix, pool_w, pool_scale.reshape(1, P))


def _prep_kernel(r_ref, k_ref, v_ref, l_ref, sr_ref, sk_ref, sv_ref, sl_ref, mr_ref, mk_ref, mv_ref, ml_ref,
                 w0_ref, w2_ref, a0_ref, a2_ref, g2_ref, kkw_ref, kaw_ref, rkw_ref,
                 kk_o, q_o, w_o, kka_o, kh_o, v_o, c2_o, bc_o, g_o, carry, *, nt, gs):
    bb, ts, C = r_ref.shape
    LW = l_ref.shape[-1]
    M = bb * ts
    t = pl.program_id(1)
    first_row = lax.broadcasted_iota(jnp.int32, (bb, ts, 1), 1) == 0

    def shifted_mix(x_ref, s_ref, m_ref, c0, width):
        x = x_ref[...]
        first = s_ref[...]
        if nt > 1:
            first = jnp.where(t == 0, first, carry[:, :, c0:c0 + width])
            carry[:, :, c0:c0 + width] = x[:, ts - 1:ts, :]
        prev = jnp.where(first_row, first, pltpu.roll(x, 1, axis=1))
        return (x + (prev - x) * m_ref[...]).reshape(M, width)

    r = shifted_mix(r_ref, sr_ref, mr_ref, 0, C)
    k = shifted_mix(k_ref, sk_ref, mk_ref, C, C)
    v = shifted_mix(v_ref, sv_ref, mv_ref, 2 * C, C)
    lo = shifted_mix(l_ref, sl_ref, ml_ref, 3 * C, LW)

    K1 = w2_ref.shape[0]
    wpre = w0_ref[...] + jnp.dot(jnp.tanh(lo[:, :K1]), w2_ref[...], precision=HIGHEST, preferred_element_type=F32)
    decay = jnp.exp(-DECAY_SCALE * jax.nn.sigmoid(wpre))
    a = jax.nn.sigmoid(a0_ref[...] + jnp.dot(lo[:, :K1], a2_ref[...], precision=HIGHEST, preferred_element_type=F32))
    g = jnp.dot(jax.nn.sigmoid(lo[:, gs:]), g2_ref[...], precision=HIGHEST, preferred_element_type=F32)

    ones = _ones_blockdiag(256 if C % 256 == 0 else LANES)
    kk = k * kkw_ref[...]
    kk = kk * lax.rsqrt(jnp.maximum(_head_sum(kk * kk, ones), 1e-24))
    kh = k * (1.0 + (a - 1.0) * kaw_ref[...])
    kka = kk * a
    c1 = _head_sum(kka * r, ones)
    rk = kh * r
    shp = (bb, ts, C)
    kk_o[...] = kk.reshape(shp)
    q_o[...] = (decay * r - kk * c1).reshape(shp)
    w_o[...] = decay.reshape(shp)
    kka_o[...] = kka.reshape(shp)
    kh_o[...] = kh.reshape(shp)
    v_o[...] = v.reshape(shp)
    c2_o[...] = _head_sum(rk, ones).reshape(shp)
    bc_o[...] = _head_sum(rk * rkw_ref[...], ones).reshape(shp)
    g_o[...] = g.reshape(shp)


def _prep(p, shift_pad, mu_pad, C, LW, w0, w2p, a0, a2p, g2p, k_k, k_a, r_k, gs):
    B, S, _ = p.shape
    bb, ts = _row_tile(B, S, 256)
    nt = S // ts
    lb = (4 * C) // LW

    def col(width, idx):
        return (pl.BlockSpec((bb, ts, width), lambda i, t: (i, t, idx)),
                pl.BlockSpec((bb, 1, width), lambda i, t: (i, 0, idx)),
                pl.BlockSpec((1, width), lambda i, t: (0, idx)))

    cols = [col(C, 1), col(C, 2), col(C, 3), col(LW, lb)]
    vec = pl.BlockSpec((1, C), lambda i, t: (0, 0))
    mat = lambda m: pl.BlockSpec(m.shape, lambda i, t: (0, 0))
    out = pl.BlockSpec((bb, ts, C), lambda i, t: (i, t, 0))
    return pl.pallas_call(
        functools.partial(_prep_kernel, nt=nt, gs=gs),
        grid=(B // bb, nt),
        in_specs=[c[0] for c in cols] + [c[1] for c in cols] + [c[2] for c in cols]
                 + [vec, mat(w2p), vec, mat(a2p), mat(g2p), vec, vec, vec],
        out_specs=[out] * 9,
        out_shape=[jax.ShapeDtypeStruct((B, S, C), F32)] * 9,
        scratch_shapes=[pltpu.VMEM((bb, 1, 3 * C + LW), F32)],
        compiler_params=_cparams(("arbitrary", "arbitrary")),
        name="rwkv_prep",
    )(p, p, p, p, shift_pad, shift_pad, shift_pad, shift_pad, mu_pad, mu_pad, mu_pad, mu_pad,
      w0.reshape(1, C), w2p, a0.reshape(1, C), a2p, g2p, k_k.reshape(1, C), k_a.reshape(1, C), r_k.reshape(1, C))


def _scan_kernel(kk_ref, q_ref, w_ref, kka_ref, kh_ref, v_ref, s0_ref, y_ref, sf_ref, s_ref):
    bb, ts, C = kk_ref.shape
    HP = C // LANES

    @pl.when(pl.program_id(1) == 0)
    def _():
        for b in range(bb):
            for hp in range(HP):
                s_ref[b, hp] = jnp.concatenate([s0_ref[b, 2 * hp], s0_ref[b, 2 * hp + 1]], axis=1)

    y_ref[...] = jnp.zeros(y_ref.shape, F32)
    GROUP = 4 if HP % 4 == 0 else (2 if HP % 2 == 0 else 1)
    ones = _ones_blockdiag(LANES)
    sum_w = jnp.concatenate([jnp.concatenate([ones, jnp.zeros_like(ones)], axis=1),
                             jnp.concatenate([ones, ones], axis=1)], axis=0)
    row = lax.broadcasted_iota(jnp.int32, (HEAD_SIZE, LANES), 0)
    lane = lax.broadcasted_iota(jnp.int32, (HEAD_SIZE, LANES), 1)
    eye = (row == lane % HEAD_SIZE).astype(F32)
    row8 = lax.broadcasted_iota(jnp.int32, (8, LANES), 0)

    def step(t, carry):
        kk_t = kk_ref[:, pl.ds(t, 1), :]
        q_t = q_ref[:, pl.ds(t, 1), :]
        w_t = w_ref[:, pl.ds(t, 1), :]
        kka_t = kka_ref[:, pl.ds(t, 1), :]
        kh_t = kh_ref[:, pl.ds(t, 1), :]
        v_t = v_ref[:, pl.ds(t, 1), :]
        t8 = pl.multiple_of((t // 8) * 8, 8)
        this_row = row8 == t - t8
        for b in range(bb):
            for g0 in range(0, HP, GROUP):
                parts = []
                for hp in range(g0, g0 + GROUP):
                    sl = slice(hp * LANES, (hp + 1) * LANES)
                    S = s_ref[b, hp]
                    sk_hi, sk_lo = _split_bf16(S * kk_t[b, :, sl])
                    parts.append(jnp.concatenate([sk_hi, sk_lo], axis=1))
                    parts.append(jnp.concatenate([(S * q_t[b, :, sl]).astype(BF16),
                                                  (eye * v_t[b, :, sl]).astype(BF16)], axis=1))
                red = jnp.dot(jnp.concatenate(parts, axis=0), sum_w, preferred_element_type=F32)
                for hp in range(g0, g0 + GROUP):
                    sl = slice(hp * LANES, (hp + 1) * LANES)
                    r0 = (hp - g0) * 2 * HEAD_SIZE
                    sa = red[r0:r0 + HEAD_SIZE, :LANES]
                    yv = red[r0 + HEAD_SIZE:r0 + 2 * HEAD_SIZE, :LANES]
                    vcol = red[r0 + HEAD_SIZE:r0 + 2 * HEAD_SIZE, LANES:]
                    s_ref[b, hp] = s_ref[b, hp] * w_t[b, :, sl] - sa * kka_t[b, :, sl] + vcol * kh_t[b, :, sl]
                    y_row = jnp.sum(yv * eye, axis=0, keepdims=True) - v_t[b, :, sl]
                    y_ref[b, pl.ds(t8, 8), sl] = jnp.where(this_row, y_row, y_ref[b, pl.ds(t8, 8), sl])
        return carry

    lax.fori_loop(0, ts, step, 0)

    @pl.when(pl.program_id(1) == pl.num_programs(1) - 1)
    def _():
        for b in range(bb):
            for hp in range(HP):
                S = s_ref[b, hp]
                sf_ref[b, 2 * hp] = S[:, :HEAD_SIZE]
                sf_ref[b, 2 * hp + 1] = S[:, HEAD_SIZE:]


def _scan(kk, q, w, kka, kh, v, s0, bb, ts):
    B, S, C = kk.shape
    HP = C // LANES
    row = pl.BlockSpec((bb, ts, C), lambda i, t: (i, t, 0))
    st = pl.BlockSpec((bb, 2 * HP, HEAD_SIZE, HEAD_SIZE), lambda i, t: (i, 0, 0, 0))
    return pl.pallas_call(
        _scan_kernel,
        grid=(B // bb, S // ts),
        in_specs=[row] * 6 + [st],
        out_specs=[row, st],
        out_shape=[jax.ShapeDtypeStruct((B, S, C), F32), jax.ShapeDtypeStruct(s0.shape, F32)],
        scratch_shapes=[pltpu.VMEM((bb, HP, HEAD_SIZE, LANES), F32)],
        compiler_params=_cparams(("arbitrary", "arbitrary")),
        name="wkv_scan",
    )(kk, q, w, kka, kh, v, s0)


def _out_kernel(x_ref, gtm_ref, shf_ref, scf_ref, pool_ref, y0_ref, v_ref, c2_ref, bc_ref, g_ref,
                lng_ref, lnb_ref, wo_ref, nf_ref, rw_ref, rb_ref,
                x1_ref, h2_ref, ti_ref, tg_ref):
    bb, ts, D = x_ref.shape
    C = y0_ref.shape[-1]
    P = pool_ref.shape[-1]
    M = bb * ts
    E = rw_ref.shape[-1]

    ones = _ones_blockdiag(256 if C % 256 == 0 else LANES)
    v = v_ref[...].reshape(M, C)
    y = y0_ref[...].reshape(M, C) + v * c2_ref[...].reshape(M, C)
    mu = _head_sum(y, ones) * (1.0 / HEAD_SIZE)
    yc = y - mu
    var = _head_sum(yc * yc, ones) * (1.0 / HEAD_SIZE)
    yn = yc * lax.rsqrt(var + LNX_EPS) * lng_ref[...] + lnb_ref[...]
    rw = (yn + bc_ref[...].reshape(M, C) * v) * g_ref[...].reshape(M, C)

    mix = (jnp.dot(pool_ref[...].reshape(M, P).astype(BF16), wo_ref[0:P, :], preferred_element_type=F32)
           + jnp.dot(rw.astype(BF16), wo_ref[P:P + C, :], preferred_element_type=F32))
    x1 = x_ref[...] + gtm_ref[...] * mix.reshape(bb, ts, D)
    x1_ref[...] = x1
    hn = x1 * lax.rsqrt(jnp.mean(x1 * x1, axis=-1, keepdims=True) + NORM_EPS) * nf_ref[...]
    h2 = (hn * (1.0 + scf_ref[...]) + shf_ref[...]).reshape(M, D)
    h2_ref[...] = h2.reshape(bb, ts, D)

    logits = jnp.dot(h2, rw_ref[...], precision=HIGHEST, preferred_element_type=F32) + rb_ref[...]
    eidx = lax.broadcasted_iota(jnp.int32, (M, E), 1)
    slot = lax.broadcasted_iota(jnp.int32, (M, LANES), 1)
    top_i = jnp.zeros((M, LANES), jnp.int32)
    top_v = jnp.full((M, LANES), -jnp.inf, F32)
    for kth in range(TOP_K):
        m = jnp.max(logits, axis=-1, keepdims=True)
        idx = jnp.min(jnp.where(logits == m, eidx, E), axis=-1, keepdims=True)
        top_i = jnp.where(slot == kth, idx, top_i)
        top_v = jnp.where(slot == kth, m, top_v)
        logits = jnp.where(eidx == idx, -jnp.inf, logits)
    ex = jnp.exp(top_v - jnp.max(top_v, axis=-1, keepdims=True))
    ti_ref[...] = top_i.reshape(bb, ts, LANES)
    tg_ref[...] = (ex / jnp.sum(ex, axis=-1, keepdims=True)).reshape(bb, ts, LANES)


def _out_proj(x, mod, pool_out, y0, v, c2, bc, g, lnx_g, lnx_b, wo_bf16, norm_g, router_w, router_b):
    B, S, D = x.shape
    C = y0.shape[-1]
    P = pool_out.shape[-1]
    E = router_w.shape[-1]
    bb, ts = _row_tile(B, S, 256)
    tok = lambda width: pl.BlockSpec((bb, ts, width), lambda i, t: (i, t, 0))
    modc = lambda idx: pl.BlockSpec((bb, 1, D), lambda i, t: (i, 0, idx))
    full = lambda a: pl.BlockSpec(a.shape, lambda i, t: (0,) * a.ndim)
    lnx_g, lnx_b, norm_g, router_b = (lnx_g.reshape(1, C), lnx_b.reshape(1, C), norm_g.reshape(1, D),
                                      router_b.reshape(1, E))
    return pl.pallas_call(
        _out_kernel,
        grid=(B // bb, S // ts),
        in_specs=[tok(D), modc(2), modc(3), modc(4), tok(P)] + [tok(C)] * 5
                 + [full(lnx_g), full(lnx_b), full(wo_bf16), full(norm_g), full(router_w), full(router_b)],
        out_specs=[tok(D), tok(D), tok(LANES), tok(LANES)],
        out_shape=[jax.ShapeDtypeStruct((B, S, D), F32), jax.ShapeDtypeStruct((B, S, D), F32),
                   jax.ShapeDtypeStruct((B, S, LANES), jnp.int32), jax.ShapeDtypeStruct((B, S, LANES), F32)],
        compiler_params=_cparams(("arbitrary", "arbitrary")),
        name="out_proj",
    )(x, mod, mod, mod, pool_out, y0, v, c2, bc, g, lnx_g, lnx_b, wo_bf16, norm_g, router_w, router_b)


def _expert_changed(be_ref, b):
    return (b == 0) | (be_ref[b] != be_ref[jnp.maximum(b - 1, 0)])


def _moe1_kernel(be_ref, nu_ref, x_ref, wg_ref, wu_ref, bg_ref, bu_ref, h_ref, wg_s, wu_s):
    b = pl.program_id(1)

    @pl.when(b < nu_ref[0])
    def _():
        @pl.when(_expert_changed(be_ref, b))
        def _():
            wg_s[...] = wg_ref[0].astype(BF16)
            wu_s[...] = wu_ref[0].astype(BF16)

        x = x_ref[...].astype(BF16)
        gate = jnp.dot(x, wg_s[...], preferred_element_type=F32) + bg_ref[0]
        up = jnp.dot(x, wu_s[...], preferred_element_type=F32) + bu_ref[0]
        gate = jnp.minimum(gate, SWIGLU_LIMIT)
        up = jnp.clip(up, -SWIGLU_LIMIT, SWIGLU_LIMIT)
        h_ref[...] = ((up + 1.0) * (gate * jax.nn.sigmoid(gate * SWIGLU_ALPHA))).astype(BF16)

    @pl.when(b >= nu_ref[0])
    def _():
        h_ref[...] = jnp.zeros(h_ref.shape, BF16)


def _moe2_kernel(be_ref, nu_ref, h_ref, wd_ref, bd_ref, y_ref, wd_s):
    b = pl.program_id(1)

    @pl.when(b < nu_ref[0])
    def _():
        @pl.when(_expert_changed(be_ref, b))
        def _():
            wd_s[...] = wd_ref[0].astype(BF16)

        y_ref[...] = jnp.dot(h_ref[...], wd_s[...], preferred_element_type=F32) + bd_ref[0]

    @pl.when(b >= nu_ref[0])
    def _():
        y_ref[...] = jnp.zeros(y_ref.shape, F32)


def _moe(xs, block_e, nused, w_gate_up, b_gate_up, w_down, b_down):
    R, D = xs.shape
    E, _, F2 = w_gate_up.shape
    F = F2 // 2
    tm = MOE_ROWS
    NB = R // tm
    tf = 512 if F % 512 == 0 else F
    NF = F // tf
    blk = lambda b, nu: jnp.minimum(b, nu[0] - 1)
    bgu = b_gate_up.reshape(E, 1, F2)
    h = pl.pallas_call(
        _moe1_kernel,
        grid_spec=pltpu.PrefetchScalarGridSpec(
            num_scalar_prefetch=2,
            grid=(NF, NB),
            in_specs=[pl.BlockSpec((tm, D), lambda f, b, be, nu: (blk(b, nu), 0)),
                      pl.BlockSpec((1, D, tf), lambda f, b, be, nu: (be[blk(b, nu)], 0, f)),
                      pl.BlockSpec((1, D, tf), lambda f, b, be, nu: (be[blk(b, nu)], 0, NF + f)),
                      pl.BlockSpec((1, 1, tf), lambda f, b, be, nu: (be[blk(b, nu)], 0, f)),
                      pl.BlockSpec((1, 1, tf), lambda f, b, be, nu: (be[blk(b, nu)], 0, NF + f))],
            out_specs=pl.BlockSpec((tm, tf), lambda f, b, be, nu: (b, f)),
            scratch_shapes=[pltpu.VMEM((D, tf), BF16), pltpu.VMEM((D, tf), BF16)]),
        out_shape=jax.ShapeDtypeStruct((R, F), BF16),
        compiler_params=_cparams(("arbitrary", "arbitrary")),
        name="moe_up",
    )(block_e, nused, xs, w_gate_up, w_gate_up, bgu, bgu)
    tn = 512 if D % 512 == 0 else D
    NN = D // tn
    return pl.pallas_call(
        _moe2_kernel,
        grid_spec=pltpu.PrefetchScalarGridSpec(
            num_scalar_prefetch=2,
            grid=(NN, NB),
            in_specs=[pl.BlockSpec((tm, F), lambda n, b, be, nu: (blk(b, nu), 0)),
                      pl.BlockSpec((1, F, tn), lambda n, b, be, nu: (be[blk(b, nu)], 0, n)),
                      pl.BlockSpec((1, 1, tn), lambda n, b, be, nu: (be[blk(b, nu)], 0, n))],
            out_specs=pl.BlockSpec((tm, tn), lambda n, b, be, nu: (b, n)),
            scratch_shapes=[pltpu.VMEM((F, tn), BF16)]),
        out_shape=jax.ShapeDtypeStruct((R, D), F32),
        compiler_params=_cparams(("arbitrary", "arbitrary")),
        name="moe_down",
    )(block_e, nused, h, w_down, b_down.reshape(E, 1, D))


def _route(top_i, n_experts):
    T = top_i.shape[0]
    A = T * TOP_K
    tm = MOE_ROWS
    NB = -(-A // tm) + n_experts
    ch = 256
    assert A % ch == 0
    onehot = (top_i.reshape(A, 1) == jnp.arange(n_experts, dtype=jnp.int32)[None, :])
    oh = onehot.astype(F32).reshape(A // ch, ch, n_experts)
    strict_lower = (jnp.arange(ch)[:, None] > jnp.arange(ch)[None, :]).astype(F32)
    within = jnp.einsum('ij,cje->cie', strict_lower, oh)
    chunk_tot = jnp.sum(oh, axis=1)
    chunk_base = jnp.cumsum(chunk_tot, axis=0) - chunk_tot
    rank = jnp.sum((within + chunk_base[:, None, :]) * oh, axis=-1).reshape(A).astype(jnp.int32)
    counts = jnp.sum(chunk_tot, axis=0).astype(jnp.int32)
    padded = ((counts + tm - 1) // tm) * tm
    pends = jnp.cumsum(padded)
    pstarts = pends - padded
    pos = jnp.sum(jnp.where(onehot, pstarts[None, :], 0), axis=-1) + rank
    block_e = jnp.sum(pends[None, :] <= (jnp.arange(NB, dtype=jnp.int32) * tm)[:, None], axis=1)
    block_e = jnp.clip(block_e, 0, n_experts - 1).astype(jnp.int32)
    nused = (pends[-1] // tm).astype(jnp.int32).reshape(1)
    pad_start = jnp.where(counts > 0, pends - tm, -1).astype(jnp.int32)
    return pos.astype(jnp.int32), block_e, nused, pad_start, NB * tm


def _dispatch_kernel(pad_ref, nu_ref, pos_ref, hp_ref, hs_ref, xs_ref, zbuf, sem, zsem, *, n_p):
    i = pl.program_id(0)
    tm, D = hp_ref.shape
    n_blocks = xs_ref.shape[0] // MOE_ROWS

    @pl.when(i == 0)
    def _():
        zbuf[...] = jnp.zeros(zbuf.shape, F32)

        def zero_block(first):
            return pltpu.make_async_copy(zbuf, xs_ref.at[pl.ds(pl.multiple_of(first, MOE_ROWS), MOE_ROWS)], zsem)

        def fill(e, c):
            @pl.when(pad_ref[e] >= 0)
            def _():
                zero_block(pad_ref[e]).start()
            return c

        def drain(e, c):
            @pl.when(pad_ref[e] >= 0)
            def _():
                zero_block(pad_ref[e]).wait()
            return c

        def fill_tail(b, c):
            zero_block(b * MOE_ROWS).start()
            return c

        def drain_tail(b, c):
            zero_block(b * MOE_ROWS).wait()
            return c

        lax.fori_loop(0, pad_ref.shape[0], fill, 0)
        lax.fori_loop(nu_ref[0], n_blocks, fill_tail, 0)
        lax.fori_loop(0, pad_ref.shape[0], drain, 0)
        lax.fori_loop(nu_ref[0], n_blocks, drain_tail, 0)

    def scatter(src):
        def row_copy(r, k):
            return pltpu.make_async_copy(src.at[pl.ds(r, 1)], xs_ref.at[pl.ds(pos_ref[0, 0, r * TOP_K + k], 1)], sem)

        def start(r, c):
            for k in range(TOP_K):
                row_copy(r, k).start()
            return c

        def wait(r, c):
            for k in range(TOP_K):
                row_copy(r, k).wait()
            return c

        lax.fori_loop(0, tm, start, 0)
        lax.fori_loop(0, tm, wait, 0)

    @pl.when(i < n_p)
    def _():
        scatter(hp_ref)

    @pl.when(i >= n_p)
    def _():
        scatter(hs_ref)


def _dispatch(h_p, h_s, pos, pad_start, nused, rows):
    Tp, D = h_p.shape
    Ts = h_s.shape[0]
    tm = 256
    assert Tp % tm == 0 and Ts % tm == 0
    n_p, n_s = Tp // tm, Ts // tm
    pos3 = pos.reshape(n_p + n_s, 1, tm * TOP_K)
    return pl.pallas_call(
        functools.partial(_dispatch_kernel, n_p=n_p),
        grid_spec=pltpu.PrefetchScalarGridSpec(
            num_scalar_prefetch=2,
            grid=(n_p + n_s,),
            in_specs=[pl.BlockSpec((1, 1, tm * TOP_K), lambda i, pad, nu: (i, 0, 0), memory_space=pltpu.SMEM),
                      pl.BlockSpec((tm, D), lambda i, pad, nu: (jnp.minimum(i, n_p - 1), 0)),
                      pl.BlockSpec((tm, D), lambda i, pad, nu: (jnp.maximum(i - n_p, 0), 0))],
            out_specs=pl.BlockSpec(memory_space=pl.ANY),
            scratch_shapes=[pltpu.VMEM((MOE_ROWS, D), F32), pltpu.SemaphoreType.DMA(()),
                            pltpu.SemaphoreType.DMA(())]),
        out_shape=jax.ShapeDtypeStruct((rows, D), F32),
        compiler_params=_cparams(("arbitrary",)),
        name="dispatch",
    )(pad_start, nused, pos3, h_p, h_s)


def _final_kernel(pos_ref, posn_ref, x1_ref, gtf_ref, tg_ref, fg_ref, y_hbm, o_ref, buf, sem):
    bb, ts, D = x1_ref.shape
    M = bb * ts
    i = pl.program_id(0)
    n = pl.num_programs(0)
    slot = i % 2

    def row_copy(idx_ref, s, r, k):
        return pltpu.make_async_copy(y_hbm.at[pl.ds(idx_ref[0, 0, r * TOP_K + k], 1)], buf.at[s, k, pl.ds(r, 1)],
                                     sem.at[s])

    def start_tile(idx_ref, s):
        def body(r, c):
            for k in range(TOP_K):
                row_copy(idx_ref, s, r, k).start()
            return c
        lax.fori_loop(0, M, body, 0)

    @pl.when(i == 0)
    def _():
        start_tile(pos_ref, 0)

    @pl.when(i + 1 < n)
    def _():
        start_tile(posn_ref, 1 - slot)

    def wait_row(r, c):
        for k in range(TOP_K):
            row_copy(pos_ref, slot, r, k).wait()
        return c
    lax.fori_loop(0, M, wait_row, 0)

    tg = tg_ref[...].reshape(M, LANES)
    ff = tg[:, 0:1] * buf[slot, 0]
    for k in range(1, TOP_K):
        ff = ff + tg[:, k:k + 1] * buf[slot, k]
    x2 = x1_ref[...] + gtf_ref[...] * ff.reshape(bb, ts, D)
    o_ref[...] = x2 * lax.rsqrt(jnp.mean(x2 * x2, axis=-1, keepdims=True) + NORM_EPS) * fg_ref[...]


def _final(x1, mod, y_slots, pos, tg, final_g):
    B, S, D = x1.shape
    bb, ts = _row_tile(B, S, 256)
    M = bb * ts
    nt = S // ts
    n = (B // bb) * nt
    pos3 = pos.reshape(n, 1, M * TOP_K)
    tok = lambda width: pl.BlockSpec((bb, ts, width), lambda i: (i // nt, i % nt, 0))
    idx = lambda f: pl.BlockSpec((1, 1, M * TOP_K), f, memory_space=pltpu.SMEM)
    return pl.pallas_call(
        _final_kernel,
        grid=(n,),
        in_specs=[idx(lambda i: (i, 0, 0)), idx(lambda i: (jnp.minimum(i + 1, n - 1), 0, 0)),
                  tok(D), pl.BlockSpec((bb, 1, D), lambda i: (i // nt, 0, 5)), tok(LANES),
                  pl.BlockSpec((1, D), lambda i: (0, 0)), pl.BlockSpec(memory_space=pl.ANY)],
        out_specs=tok(D),
        out_shape=jax.ShapeDtypeStruct((B, S, D), F32),
        scratch_shapes=[pltpu.VMEM((2, TOP_K, M, D), F32), pltpu.SemaphoreType.DMA((2,))],
        compiler_params=_cparams(("arbitrary",)),
        name="final",
    )(pos3, pos3, x1, mod, tg, final_g.reshape(1, D), y_slots)


def _mixer(x, mod, prefix, shift_prev, wkv0, start_pos, prm, scan_tile):
    B, S, D = x.shape
    C, P, LW, RW = prm['C'], prm['P'], prm['LW'], prm['RW']
    p = _in_proj(x, mod, prm['norm_mix_g'], prm['w_in'], prm['tn_in'])
    pool_out = _pool(p, prefix, prm['pool_w'], prm['pool_scale'], start_pos)
    shift_pad = jnp.pad(shift_prev, ((0, 0), (P, p.shape[-1] - P - RW)))[:, None, :]
    kk, q, w, kka, kh, v, c2, bc, g = _prep(p, shift_pad, prm['mu_pad'], C, LW, prm['w0'], prm['w2p'], prm['a0'],
                                            prm['a2p'], prm['g2p'], prm['k_k'], prm['k_a'], prm['r_k'], prm['gs'])
    y0, s_fin = _scan(kk, q, w, kka, kh, v, wkv0, *scan_tile)
    x1, h2, top_i, top_g = _out_proj(x, mod, pool_out, y0, v, c2, bc, g, prm['lnx_g'], prm['lnx_b'], prm['w_out'],
                                     prm['norm_ffn_g'], prm['router_w'], prm['router_b'])
    new_pool = jnp.concatenate([prefix, p[:, :, :P]], axis=1)[:, -POOL_BUF:]
    new_shift = p[:, -1, P:P + RW]
    return x1, h2, top_i, top_g, new_pool, new_shift, s_fin


def kernel(x_prompt, x_sample, c_prompt, c_sample, state_pool, state_shift, state_wkv, norm_mix_g, norm_ffn_g, final_g, w_ada, b_ada, w_in, pool_w, pool_scale, mu_shift, w0, w2, a0, a2, g2, k_k, k_a, r_k, lnx_g, lnx_b, w_out, router_w, router_b, w_gate_up, b_gate_up, w_down, b_down):
    Bp, Sp, D = x_prompt.shape
    Bs, Ss, _ = x_sample.shape
    depth = w_in.shape[0]
    assert depth == 1, "the final rmsnorm is fused into the layer's last kernel; only one layer is supported"
    P = pool_w.shape[1] * pool_w.shape[2]
    C = w0.shape[1]
    H = C // HEAD_SIZE
    RW = mu_shift.shape[1]
    lw, la, lg = w2.shape[1], a2.shape[1], g2.shape[1]
    assert P == C and C % LANES == 0 and RW == 3 * C + lw + la + lg
    E = router_w.shape[-1]
    LW = next(n for n in range(-(-(lw + la + lg) // LANES) * LANES, 4 * C + 1, LANES) if (4 * C) % n == 0)
    NP = 4 * C + LW
    tn_in = NP // 3 if NP % (3 * LANES) == 0 else NP
    K1 = min(LW, -(-(lw + la) // LANES) * LANES)
    gs = ((lw + la) // LANES) * LANES

    xp, xs = x_prompt, x_sample
    outs = [[] for _ in range(6)]
    for l in range(depth):
        prm = {
            'C': C, 'P': P, 'LW': LW, 'RW': RW, 'tn_in': tn_in, 'gs': gs,
            'norm_mix_g': norm_mix_g[l].reshape(1, D), 'norm_ffn_g': norm_ffn_g[l],
            'w_in': jnp.pad(w_in[l].astype(BF16), ((0, 0), (0, NP - P - RW))),
            'pool_w': pool_w[l], 'pool_scale': pool_scale[l],
            'mu_pad': jnp.pad(mu_shift[l], (P, NP - P - RW)).reshape(1, NP),
            'w0': w0[l], 'a0': a0[l], 'k_k': k_k[l], 'k_a': k_a[l], 'r_k': r_k[l].reshape(C),
            'w2p': jnp.pad(w2[l], ((0, K1 - lw), (0, 0))),
            'a2p': jnp.pad(a2[l], ((lw, K1 - lw - la), (0, 0))),
            'g2p': jnp.pad(g2[l], ((lw + la - gs, LW - lw - la - lg), (0, 0))),
            'lnx_g': lnx_g[l], 'lnx_b': lnx_b[l], 'w_out': w_out[l].astype(BF16),
            'router_w': router_w[l], 'router_b': router_b[l],
        }
        mod = _ada(jnp.concatenate([c_prompt, c_sample], axis=0), w_ada[l], b_ada[l])
        mod_p, mod_s = mod[:Bp, None, :], mod[Bp:, None, :]
        zero_pool = jnp.zeros((Bp, POOL_BUF, P), F32)
        zero_shift = jnp.zeros((Bp, RW), F32)
        zero_wkv = jnp.zeros((Bp, H, HEAD_SIZE, HEAD_SIZE), F32)
        rp = _mixer(xp, mod_p, zero_pool, zero_shift, zero_wkv, 0, prm, (Bp, min(Sp, 128)))
        rs = _mixer(xs, mod_s, state_pool[l], state_shift[l], state_wkv[l], PAST_LEN, prm, (min(Bs, 8), Ss))

        Tp, Ts = Bp * Sp, Bs * Ss
        top_i = jnp.concatenate([rp[2].reshape(Tp, LANES)[:, :TOP_K], rs[2].reshape(Ts, LANES)[:, :TOP_K]], axis=0)
        pos, block_e, nused, pad_start, rows = _route(top_i, E)
        x_slots = _dispatch(rp[1].reshape(Tp, D), rs[1].reshape(Ts, D), pos, pad_start, nused, rows)
        y_slots = _moe(x_slots, block_e, nused, w_gate_up[l], b_gate_up[l], w_down[l], b_down[l])
        xp = _final(rp[0], mod_p, y_slots, pos[:Tp * TOP_K], rp[3], final_g)
        xs = _final(rs[0], mod_s, y_slots, pos[Tp * TOP_K:], rs[3], final_g)
        for lst, val in zip(outs, (rp[4], rp[5], rp[6], rs[4], rs[5], rs[6])):
            lst.append(val)
    return (xp, xs) + tuple(jnp.stack(o) for o in outs)
```
